```python
import jax, jax.numpy as jnp
from jax import lax
import numpy as np

D_MODEL = 1024
BATCH = 32
SEQ = 256
DEPTH = 4
DEC_BATCH = 8
DEC_SEQ = 1024
PAST_LEN = 256

GRID_W = 64
N_EVEN = (DEPTH + 1) // 2
N_ODD = DEPTH // 2
H_A = 4
DK = 128
DV = 128
DA = H_A * DV
CB = D_MODEL // 2
CHUNK = 64
QKV_CONV_W = 3
CONF_CONV_W = 31
DC = D_MODEL
SHORT_CONV_W = 3
N_EXPERTS = 32
TOP_K = 4
D_FF = D_MODEL
SWIGLU_LIMIT = 7.0
SWIGLU_ALPHA = 1.702
EPS = 1e-6
P_EVEN = 4 * DA + 4 * H_A + 2 * CB

kernel_name = 'hybrid_deltanet_conformer_shortconv_moe_prefix_step'


def _rms_norm(x, g):
    xf = x.astype(jnp.float32)
    y = xf * lax.rsqrt(jnp.mean(xf * xf, axis=-1, keepdims=True) + EPS)
    return (y * g.astype(jnp.float32)).astype(x.dtype)


def _layer_norm(x, g, b):
    xf = x.astype(jnp.float32)
    mu = jnp.mean(xf, axis=-1, keepdims=True)
    xc = xf - mu
    y = xc * lax.rsqrt(jnp.mean(xc * xc, axis=-1, keepdims=True) + EPS)
    return (y * g.astype(jnp.float32) + b.astype(jnp.float32)).astype(x.dtype)


def _l2norm(x):
    return x * lax.rsqrt(jnp.sum(x * x, axis=-1, keepdims=True) + EPS)


def _dwconv(x, w):
    k = w.shape[0]
    return lax.conv_general_dilated(
        x, w[:, None, :].astype(x.dtype), window_strides=(1,),
        padding=[(k // 2, k // 2)], dimension_numbers=('NWC', 'WIO', 'NWC'),
        feature_group_count=x.shape[-1])


def _row_conv(u, w):
    b, t, c = u.shape
    rows = t // GRID_W
    return _dwconv(u.reshape(b * rows, GRID_W, c), w).reshape(b, t, c)


def _col_conv(u, w):
    b, t, c = u.shape
    rows = t // GRID_W
    g = u.reshape(b, rows, GRID_W, c).transpose(0, 2, 1, 3).reshape(b * GRID_W, rows, c)
    g = _dwconv(g, w)
    return g.reshape(b, GRID_W, rows, c).transpose(0, 2, 1, 3).reshape(b, t, c)


def _gated_delta_chunked(q, k, v, g, beta, s0):
    b, h, t, dk = q.shape
    dv = v.shape[-1]
    n = t // CHUNK
    q = (q * dk ** -0.5).reshape(b, h, n, CHUNK, dk)
    k = k.reshape(b, h, n, CHUNK, dk)
    v = v.reshape(b, h, n, CHUNK, dv)
    beta = beta.reshape(b, h, n, CHUNK)
    gc = jnp.cumsum(g.reshape(b, h, n, CHUNK), axis=-1)
    idx = jnp.arange(CHUNK)
    causal = idx[:, None] >= idx[None, :]
    strict = idx[:, None] > idx[None, :]
    diff = gc[..., :, None] - gc[..., None, :]
    decay = jnp.where(causal, jnp.exp(jnp.where(causal, diff, 0.0)), 0.0)
    kb = k * beta[..., None]
    lmat = jnp.where(strict, -jnp.einsum('bhncd,bhnsd->bhncs', kb, k) * decay, 0.0)
    eye = jnp.eye(CHUNK, dtype=jnp.float32)
    a = eye - lmat
    tmat = lax.linalg.triangular_solve(a, jnp.broadcast_to(eye, a.shape),
                                       left_side=True, lower=True, unit_diagonal=True)
    u = tmat @ (v * beta[..., None])
    w = tmat @ (kb * jnp.exp(gc)[..., None])
    attn = jnp.einsum('bhncd,bhnsd->bhncs', q, k) * decay
    q_dec = q * jnp.exp(gc)[..., None]
    k_dec = k * jnp.exp(gc[..., -1:] - gc)[..., None]
    g_last = jnp.exp(gc[..., -1])

    def step(s, inp):
        qd, kd, ui, wi, ai, gl = inp
        v_new = ui - wi @ s
        o = qd @ s + ai @ v_new
        s = s * gl[..., None, None] + jnp.swapaxes(kd, -1, -2) @ v_new
        return s, o

    xs = tuple(jnp.moveaxis(z, 2, 0) for z in (q_dec, k_dec, u, w, attn, g_last))
    s_final, o = lax.scan(step, s0, xs)
    o = jnp.moveaxis(o, 0, 2).reshape(b, h, t, dv)
    return o, s_final


def _bidir_delta(q, k, v, g_f, g_b, beta_f, beta_b, s_f, s_b):
    o_f, sf = _gated_delta_chunked(q, k, v, g_f, beta_f, s_f)
    o_b, sb = _gated_delta_chunked(jnp.flip(q, 2), jnp.flip(k, 2), jnp.flip(v, 2),
                                   jnp.flip(g_b, 2), jnp.flip(beta_b, 2), s_b)
    return o_f + jnp.flip(o_b, 2), sf, sb


def _even_mixer(h, s_f, s_b, w_in, conv_qkv, a_log, dt_bias, o_norm, dw_conv, dw_bias,
                ln_g, ln_b, w_out, latent):
    b, t, _ = h.shape
    f32 = jnp.float32
    cuts = [int(i) for i in np.cumsum([3 * DA, DA, H_A, H_A, H_A, H_A, CB])]
    qkv, gate, a_f, a_b, b_f, b_b, glu_v, glu_g = jnp.split(h @ w_in, cuts, axis=-1)
    qkv = jax.nn.silu(_dwconv(qkv, conv_qkv)).astype(f32)
    heads = lambda z: z.reshape(b, t, H_A, -1).transpose(0, 2, 1, 3)
    q = _l2norm(heads(qkv[..., :DA]))
    k = _l2norm(heads(qkv[..., DA:2 * DA]))
    v = heads(qkv[..., 2 * DA:])

    def log_decay(a, j):
        gd = -jnp.exp(a_log[j].astype(f32)) * jax.nn.softplus(a.astype(f32) + dt_bias[j].astype(f32))
        return gd.transpose(0, 2, 1)

    beta_f = jax.nn.sigmoid(b_f.astype(f32)).transpose(0, 2, 1)
    beta_b = jax.nn.sigmoid(b_b.astype(f32)).transpose(0, 2, 1)
    o, sf, sb = _bidir_delta(q, k, v, log_decay(a_f, 0), log_decay(a_b, 1), beta_f, beta_b, s_f, s_b)
    o = _rms_norm(o.transpose(0, 2, 1, 3), o_norm) * jax.nn.silu(gate.reshape(b, t, H_A, DV).astype(f32))
    o = o.reshape(b, t, DA).astype(h.dtype)
    cu = glu_v * jax.nn.sigmoid(glu_g)
    cu = (_row_conv(cu, dw_conv) if latent else _dwconv(cu, dw_conv)) + dw_bias
    cu = jax.nn.silu(_layer_norm(cu, ln_g, ln_b))
    y = jnp.concatenate([o, cu], axis=-1) @ w_out
    return y, sf, sb


def _shortconv_mixer(h, w_in, w_conv, w_out, latent):
    bg, cg, xh = jnp.split(h @ w_in, 3, axis=-1)
    u = cg * xh
    u = _col_conv(u, w_conv) if latent else _dwconv(u, w_conv)
    return (bg * u) @ w_out


def _moe(h, rw, rb, wgu, bgu, wd, bd):
    b, t, d = h.shape
    x = h.reshape(b * t, d)
    logits = (x @ rw + rb).astype(jnp.float32)
    top_v, top_i = lax.top_k(logits, TOP_K)
    wts = jax.nn.softmax(top_v, axis=-1)
    comb = jnp.einsum('nk,nke->ne', wts,
                      jax.nn.one_hot(top_i, N_EXPERTS, dtype=jnp.float32)).astype(x.dtype)
    out = jnp.zeros_like(x)
    for e in range(N_EXPERTS):
        gte, up = jnp.split(x @ wgu[e] + bgu[e], 2, axis=-1)
        gte = jnp.minimum(gte, SWIGLU_LIMIT)
        up = jnp.clip(up, -SWIGLU_LIMIT, SWIGLU_LIMIT)
        act = (up + 1.0) * gte * jax.nn.sigmoid(SWIGLU_ALPHA * gte)
        out = out + comb[:, e:e + 1] * (act @ wd[e] + bd[e])
    return out.reshape(b, t, d)


def _trunk(x, cond, states, p, latent):
    f32 = jnp.float32
    new_states = []
    for l in range(DEPTH):
        j = l // 2
        mod = (jax.nn.silu(cond) @ p['w_mod'][l] + p['b_mod'][l])[:, None, :]
        sh1, sc1, g1, sh2, sc2, g2 = jnp.split(mod, 6, axis=-1)
        h = _rms_norm(x, p['norm_mix'][l]) * (1.0 + sc1) + sh1
        if l % 2 == 0:
            if latent:
                s_f = states[:, j, 0].astype(f32)
                s_b = states[:, j, 1].astype(f32)
            else:
                s_f = jnp.zeros((x.shape[0], H_A, DK, DV), f32)
                s_b = s_f
            y, sf, sb = _even_mixer(h, s_f, s_b, p['w_in_even'][j], p['conv_qkv'][j], p['a_log'][j],
                                    p['dt_bias'][j], p['o_norm'][j], p['dw_conv'][j], p['dw_bias'][j],
                                    p['ln_g'][j], p['ln_b'][j], p['w_out_even'][j], latent)
            if not latent:
                new_states.append(jnp.stack([sf, sb], axis=1))
        else:
            y = _shortconv_mixer(h, p['w_in_odd'][j], p['conv_odd'][j], p['w_out_odd'][j], latent)
        x = x + g1 * y
        h = _rms_norm(x, p['norm_ffn'][l]) * (1.0 + sc2) + sh2
        x = x + g2 * _moe(h, p['router_w'][l], p['router_b'][l], p['w_gate_up'][l],
                          p['b_gate_up'][l], p['w_down'][l], p['b_down'][l])
    y_out = _rms_norm(x, p['final_norm'])
    if latent:
        return y_out, None
    return y_out, jnp.stack(new_states, axis=1).astype(x.dtype)


def setup_inputs(seed: int = 0) -> dict:
    key = jax.random.key(seed)
    ks = iter(jax.random.split(key, 40))
    f = jnp.float32
    nrm = lambda shape, scale: jax.random.normal(next(ks), shape, f) * scale
    dt = jnp.exp(jax.random.uniform(next(ks), (N_EVEN, 2, H_A), f, np.log(1e-3), np.log(1e-1)))
    a_log = jnp.log(jax.random.uniform(next(ks), (N_EVEN, 2, H_A), f, 1.0, 16.0))
    return {
        'x_prompt': nrm((BATCH, SEQ, D_MODEL), 1.0),
        'x_sample': nrm((DEC_BATCH, DEC_SEQ, D_MODEL), 1.0),
        'state_delta': nrm((DEC_BATCH, N_EVEN, 2, H_A, DK, DV), 0.1),
        'c': nrm((DEC_BATCH, D_MODEL), 1.0),
        'c_ctx': nrm((D_MODEL,), 1.0),
        'w_mod': nrm((DEPTH, D_MODEL, 6 * D_MODEL), 0.5 * D_MODEL ** -0.5),
        'b_mod': nrm((DEPTH, 6 * D_MODEL), 0.02),
        'norm_mix': 1.0 + nrm((DEPTH, D_MODEL), 0.02),
        'norm_ffn': 1.0 + nrm((DEPTH, D_MODEL), 0.02),
        'w_in_even': nrm((N_EVEN, D_MODEL, P_EVEN), D_MODEL ** -0.5),
        'conv_qkv': nrm((N_EVEN, QKV_CONV_W, 3 * DA), QKV_CONV_W ** -0.5),
        'a_log': a_log,
        'dt_bias': dt + jnp.log(-jnp.expm1(-dt)),
        'o_norm': 1.0 + nrm((N_EVEN, DV), 0.02),
        'dw_conv': nrm((N_EVEN, CONF_CONV_W, CB), CONF_CONV_W ** -0.5),
        'dw_bias': nrm((N_EVEN, CB), 0.02),
        'ln_g': 1.0 + nrm((N_EVEN, CB), 0.02),
        'ln_b': nrm((N_EVEN, CB), 0.02),
        'w_out_even': nrm((N_EVEN, DA + CB, D_MODEL), (DA + CB) ** -0.5),
        'w_in_odd': nrm((N_ODD, D_MODEL, 3 * DC), D_MODEL ** -0.5),
        'conv_odd': nrm((N_ODD, SHORT_CONV_W, DC), SHORT_CONV_W ** -0.5),
        'w_out_odd': nrm((N_ODD, DC, D_MODEL), DC ** -0.5),
        'router_w': nrm((DEPTH, D_MODEL, N_EXPERTS), D_MODEL ** -0.5),
        'router_b': nrm((DEPTH, N_EXPERTS), 0.01),
        'w_gate_up': nrm((DEPTH, N_EXPERTS, D_MODEL, 2 * D_FF), D_MODEL ** -0.5),
        'b_gate_up': nrm((DEPTH, N_EXPERTS, 2 * D_FF), 0.02),
        'w_down': nrm((DEPTH, N_EXPERTS, D_FF, D_MODEL), D_FF ** -0.5),
        'b_down': nrm((DEPTH, N_EXPERTS, D_MODEL), 0.02),
        'final_norm': 1.0 + nrm((D_MODEL,), 0.02),
    }


def reference(x_prompt, x_sample, state_delta, c, c_ctx, w_mod, b_mod, norm_mix, norm_ffn,
              w_in_even, conv_qkv, a_log, dt_bias, o_norm, dw_conv, dw_bias, ln_g, ln_b,
              w_out_even, w_in_odd, conv_odd, w_out_odd, router_w, router_b, w_gate_up,
              b_gate_up, w_down, b_down, final_norm):
    p = {'w_mod': w_mod, 'b_mod': b_mod, 'norm_mix': norm_mix, 'norm_ffn': norm_ffn,
         'w_in_even': w_in_even, 'conv_qkv': conv_qkv, 'a_log': a_log, 'dt_bias': dt_bias,
         'o_norm': o_norm, 'dw_conv': dw_conv, 'dw_bias': dw_bias, 'ln_g': ln_g, 'ln_b': ln_b,
         'w_out_even': w_out_even, 'w_in_odd': w_in_odd, 'conv_odd': conv_odd,
         'w_out_odd': w_out_odd, 'router_w': router_w, 'router_b': router_b,
         'w_gate_up': w_gate_up, 'b_gate_up': b_gate_up, 'w_down': w_down, 'b_down': b_down,
         'final_norm': final_norm}
    y_prompt, new_state_delta = _trunk(x_prompt, c_ctx[None, :], None, p, False)
    y_sample, _ = _trunk(x_sample, c, state_delta, p, True)
    return (y_prompt, y_sample, new_state_delta)
```

```python
import functools

import jax
import jax.numpy as jnp
from jax import lax
from jax.experimental import pallas as pl
from jax.experimental.pallas import tpu as pltpu

F32 = jnp.float32
BF16 = jnp.bfloat16
I32 = jnp.int32

EPS = 1e-6
GRID_W = 64
H_A = 4
DK = 128
CHUNK = 64
QKV_CONV_W = 3
CONF_CONV_W = 31
N_EXPERTS = 32
TOP_K = 4
SWIGLU_LIMIT = 7.0
SWIGLU_ALPHA = 1.702
LANES = 128
ROW_TILE = 256
MOE_TILE = 256
COMBINE_TILE = 64
VMEM_LIMIT = 56 * 1024 * 1024


def _cparams(n_axes=1):
    return pltpu.CompilerParams(dimension_semantics=("arbitrary",) * n_axes,
                                vmem_limit_bytes=VMEM_LIMIT)


def _sigmoid(x):
    return 1.0 / (1.0 + jnp.exp(-x))


def _silu(x):
    return x * _sigmoid(x)


def _softplus(x):
    return jnp.maximum(x, 0.0) + jnp.log1p(jnp.exp(-jnp.abs(x)))


def _dot(a, b):
    return jnp.dot(a, b, preferred_element_type=F32)


def _dot_nt(a, b):
    return lax.dot_general(a, b, (((1,), (1,)), ((), ())), preferred_element_type=F32)


def _split_bf16(a):
    hi = a.astype(BF16)
    return hi, (a - hi.astype(F32)).astype(BF16)


def _dot3(a, b, nt=False):
    dot = _dot_nt if nt else _dot
    a_hi, a_lo = _split_bf16(a)
    b_hi, b_lo = _split_bf16(b)
    return dot(a_hi, b_hi) + (dot(a_hi, b_lo) + dot(a_lo, b_hi))


def _norm_mod(x, nw, shift, scale):
    ms = jnp.mean(x * x, axis=-1, keepdims=True)
    return (x * lax.rsqrt(ms + EPS) * nw) * (1.0 + scale) + shift


def _cond_index(n_ctx_rows, seq_lat, tile):
    n_ctx_blocks = n_ctx_rows // tile
    per_seq = seq_lat // tile

    def f(i):
        return jnp.where(i < n_ctx_blocks, 0, 1 + (i - n_ctx_blocks) // per_seq)
    return f


def _mod_kernel(c_ref, w_ref, b_ref, o_ref):
    s = _silu(c_ref[...]).astype(BF16)
    o_ref[...] = _dot(s, w_ref[...].astype(BF16)) + b_ref[...]


def _modulation(cond, w_mod, b_mod):
    depth, d, n6 = w_mod.shape
    rows = cond.shape[0]
    tn = 1536
    return pl.pallas_call(
        _mod_kernel,
        grid=(depth, n6 // tn),
        in_specs=[pl.BlockSpec((rows, d), lambda l, j: (0, 0)),
                  pl.BlockSpec((None, d, tn), lambda l, j: (l, 0, j)),
                  pl.BlockSpec((None, 1, tn), lambda l, j: (l, 0, j))],
        out_specs=pl.BlockSpec((None, rows, tn), lambda l, j: (l, 0, j)),
        out_shape=jax.ShapeDtypeStruct((depth, rows, n6), F32),
        compiler_params=_cparams(2),
        name="modulation",
    )(cond, w_mod, b_mod.reshape(depth, 1, n6))


def _proj_kernel(x_ref, mod_ref, nw_ref, w_ref, o_ref):
    h = _norm_mod(x_ref[...], nw_ref[...], mod_ref[0:1, :], mod_ref[1:2, :])
    o_ref[...] = _dot(h.astype(BF16), w_ref[...])


def _projection(x, mods, nw, w, cond_of):
    n, d = x.shape
    nout = w.shape[1]
    return pl.pallas_call(
        _proj_kernel,
        grid=(n // ROW_TILE,),
        in_specs=[pl.BlockSpec((ROW_TILE, d), lambda i: (i, 0)),
                  pl.BlockSpec((None, 6, d), lambda i: (cond_of(i), 0, 0)),
                  pl.BlockSpec((1, d), lambda i: (0, 0)),
                  pl.BlockSpec((d, nout), lambda i: (0, 0))],
        out_specs=pl.BlockSpec((ROW_TILE, nout), lambda i: (i, 0)),
        out_shape=jax.ShapeDtypeStruct((n, nout), F32),
        compiler_params=_cparams(),
        name="norm_proj",
    )(x, mods, nw, w)


def _unit_tri_inverse(lmat, c):
    ii = lax.broadcasted_iota(I32, (c, c), 0)
    jj = lax.broadcasted_iota(I32, (c, c), 1)
    x = jnp.where(ii == jj, 1.0, 0.0) + jnp.where(ii // 2 == jj // 2, lmat, 0.0)
    b = 2
    while b < c:
        off = jnp.where((ii // (2 * b) == jj // (2 * b)) & (ii // b != jj // b), lmat, 0.0)
        x = x + _dot3(_dot3(x, off), x)
        b *= 2
    return x


def _delta_kernel(qkv_ref, gate_ref, small_ref, cw_ref, gp_ref, on_ref, s0_ref,
                  o_ref, sfin_ref, q_s, k_s, v_s, gc_s, beta_s, oacc_s, st_s, *, seq, chunk):
    n_chunks = seq // chunk
    row = lax.broadcasted_iota(I32, (seq, 1), 0)

    for j in range(3 * H_A):
        x = qkv_ref[:, j * DK:(j + 1) * DK]
        cw = cw_ref[:, j * DK:(j + 1) * DK]
        xp = jnp.where(row >= 1, pltpu.roll(x, 1, 0), 0.0)
        xn = jnp.where(row <= seq - 2, pltpu.roll(x, seq - 1, 0), 0.0)
        y = _silu(xp * cw[0:1, :] + x * cw[1:2, :] + xn * cw[2:3, :])
        if j < 2 * H_A:
            y = y * lax.rsqrt(jnp.sum(y * y, axis=-1, keepdims=True) + EPS)
        if j < H_A:
            q_s[j] = y * (DK ** -0.5)
        elif j < 2 * H_A:
            k_s[j - H_A] = y
        else:
            v_s[j - 2 * H_A] = y

    sm = small_ref[...]
    g = -jnp.exp(gp_ref[0:1, :]) * _softplus(sm + gp_ref[1:2, :])
    beta_s[...] = _sigmoid(sm)
    pos = row % chunk
    gf = g
    gb = g
    s = 1
    while s < chunk:
        gf = gf + jnp.where(pos >= s, pltpu.roll(gf, s, 0), 0.0)
        gb = gb + jnp.where(pos <= chunk - 1 - s, pltpu.roll(gb, seq - s, 0), 0.0)
        s *= 2
    lane = lax.broadcasted_iota(I32, (seq, LANES), 1)
    gc_s[...] = jnp.where(lane < H_A, gf, gb)

    st_s[...] = s0_ref[...]
    oacc_s[...] = jnp.zeros_like(oacc_s)
    ii = lax.broadcasted_iota(I32, (chunk, chunk), 0)
    jj = lax.broadcasted_iota(I32, (chunk, chunk), 1)

    def chunk_step(c, carry):
        for d in range(2):
            cc = c if d == 0 else n_chunks - 1 - c
            r0 = pl.multiple_of(cc * chunk, chunk)
            gcb = gc_s[pl.ds(r0, chunk), :]
            btb = beta_s[pl.ds(r0, chunk), :]
            gct = gcb.T
            mask = (ii >= jj) if d == 0 else (ii <= jj)
            strict = (ii > jj) if d == 0 else (ii < jj)
            last = chunk - 1 if d == 0 else 0
            for h in range(H_A):
                l = d * H_A + h
                gcol = gcb[:, l:l + 1]
                grow = gct[l:l + 1, :]
                decay = jnp.where(mask, jnp.exp(jnp.where(mask, gcol - grow, 0.0)), 0.0)
                bcol = btb[:, 2 * H_A + l:2 * H_A + l + 1]
                q = q_s[h, pl.ds(r0, chunk), :]
                k = k_s[h, pl.ds(r0, chunk), :]
                v = v_s[h, pl.ds(r0, chunk), :]
                kb = k * bcol
                lmat = jnp.where(strict, -_dot3(kb, k, nt=True) * decay, 0.0)
                tmat = _unit_tri_inverse(lmat, chunk)
                eg = jnp.exp(gcol)
                u = _dot3(tmat, v * bcol)
                w = _dot3(tmat, kb * eg)
                attn = _dot3(q, k, nt=True) * decay
                glast = gcb[last:last + 1, l:l + 1]
                kd = k * jnp.exp(glast - gcol)
                st = st_s[d, h]
                v_new = u - _dot3(w, st)
                o = _dot3(q * eg, st) + _dot3(attn, v_new)
                st_s[d, h] = st * jnp.exp(glast) + _dot3(kd.T, v_new)
                oacc_s[pl.ds(r0, chunk), h * DK:(h + 1) * DK] += o
        return carry

    lax.fori_loop(0, n_chunks, chunk_step, 0)

    sfin_ref[...] = st_s[...]
    for h in range(H_A):
        o = oacc_s[:, h * DK:(h + 1) * DK]
        o = o * lax.rsqrt(jnp.mean(o * o, axis=-1, keepdims=True) + EPS) * on_ref[...]
        o_ref[:, h * DK:(h + 1) * DK] = (o * _silu(gate_ref[:, h * DK:(h + 1) * DK])).astype(BF16)


def _delta_mixer(proj, conv_qkv, gate_params, o_norm, s0, seq, row_block_off):
    n_seq = s0.shape[0]
    da = H_A * DK
    kern = functools.partial(_delta_kernel, seq=seq, chunk=CHUNK)
    return pl.pallas_call(
        kern,
        grid=(n_seq,),
        in_specs=[pl.BlockSpec((seq, 3 * da), lambda b: (b + row_block_off, 0)),
                  pl.BlockSpec((seq, da), lambda b: (b + row_block_off, 3)),
                  pl.BlockSpec((seq, LANES), lambda b: (b + row_block_off, (6 * da) // LANES)),
                  pl.BlockSpec((QKV_CONV_W, 3 * da), lambda b: (0, 0)),
                  pl.BlockSpec((8, LANES), lambda b: (0, 0)),
                  pl.BlockSpec((1, DK), lambda b: (0, 0)),
                  pl.BlockSpec((None, 2, H_A, DK, DK), lambda b: (b, 0, 0, 0, 0))],
        out_specs=[pl.BlockSpec((seq, da), lambda b: (b, 0)),
                   pl.BlockSpec((None, 2, H_A, DK, DK), lambda b: (b, 0, 0, 0, 0))],
        out_shape=[jax.ShapeDtypeStruct((n_seq * seq, da), BF16),
                   jax.ShapeDtypeStruct((n_seq, 2, H_A, DK, DK), F32)],
        scratch_shapes=[pltpu.VMEM((H_A, seq, DK), F32), pltpu.VMEM((H_A, seq, DK), F32),
                        pltpu.VMEM((H_A, seq, DK), F32), pltpu.VMEM((seq, LANES), F32),
                        pltpu.VMEM((seq, LANES), F32), pltpu.VMEM((seq, da), F32),
                        pltpu.VMEM((2, H_A, DK, DK), F32)],
        compiler_params=_cparams(),
        name="delta_mixer",
    )(proj, proj, proj, conv_qkv, gate_params, o_norm, s0)


def _conf_kernel(v_ref, g_ref, w_ref, b_ref, lg_ref, lb_ref, o_ref, *, seg):
    rows = v_ref.shape[0]
    cu = v_ref[...] * _sigmoid(g_ref[...])
    pos = lax.broadcasted_iota(I32, (rows, 1), 0) % seg
    half = CONF_CONV_W // 2
    acc = cu * w_ref[half:half + 1, :]
    for k in range(CONF_CONV_W):
        s = k - half
        if s == 0 or abs(s) >= seg:
            continue
        shifted = pltpu.roll(cu, (-s) % rows, 0)
        valid = (pos >= -s) if s < 0 else (pos < seg - s)
        acc = acc + jnp.where(valid, shifted, 0.0) * w_ref[k:k + 1, :]
    acc = acc + b_ref[...]
    mu = jnp.mean(acc, axis=-1, keepdims=True)
    xc = acc - mu
    y = xc * lax.rsqrt(jnp.mean(xc * xc, axis=-1, keepdims=True) + EPS)
    o_ref[...] = _silu(y * lg_ref[...] + lb_ref[...]).astype(BF16)


def _conformer(proj, dw_conv, dw_bias, ln_g, ln_b, n_rows, seg, row_block_off):
    cb = dw_conv.shape[1]
    kern = functools.partial(_conf_kernel, seg=seg)
    vcol = (4 * H_A * DK) // cb
    return pl.pallas_call(
        kern,
        grid=(n_rows // ROW_TILE,),
        in_specs=[pl.BlockSpec((ROW_TILE, cb), lambda i: (i + row_block_off, vcol)),
                  pl.BlockSpec((ROW_TILE, cb), lambda i: (i + row_block_off, vcol + 1)),
                  pl.BlockSpec((CONF_CONV_W, cb), lambda i: (0, 0)),
                  pl.BlockSpec((1, cb), lambda i: (0, 0)),
                  pl.BlockSpec((1, cb), lambda i: (0, 0)),
                  pl.BlockSpec((1, cb), lambda i: (0, 0))],
        out_specs=pl.BlockSpec((ROW_TILE, cb), lambda i: (i, 0)),
        out_shape=jax.ShapeDtypeStruct((n_rows, cb), BF16),
        compiler_params=_cparams(),
        name="conformer_conv",
    )(proj, proj, dw_conv, dw_bias, ln_g, ln_b)


def _out_kernel(*refs, n_in):
    x_ref, mod_ref = refs[0], refs[1]
    a_refs = refs[2:2 + n_in]
    w_ref = refs[2 + n_in]
    o_ref = refs[3 + n_in]
    y = None
    k0 = 0
    for a_ref in a_refs:
        kw = a_ref.shape[1]
        part = _dot(a_ref[...], w_ref[k0:k0 + kw, :])
        y = part if y is None else y + part
        k0 += kw
    o_ref[...] = x_ref[...] + mod_ref[2:3, :] * y


def _out_projection(x, mods, acts, w, cond_of):
    n, d = x.shape
    kern = functools.partial(_out_kernel, n_in=len(acts))
    return pl.pallas_call(
        kern,
        grid=(n // ROW_TILE,),
        in_specs=([pl.BlockSpec((ROW_TILE, d), lambda i: (i, 0)),
                   pl.BlockSpec((None, 6, d), lambda i: (cond_of(i), 0, 0))]
                  + [pl.BlockSpec((ROW_TILE, a.shape[1]), lambda i: (i, 0)) for a in acts]
                  + [pl.BlockSpec(w.shape, lambda i: (0, 0))]),
        out_specs=pl.BlockSpec((ROW_TILE, d), lambda i: (i, 0)),
        out_shape=jax.ShapeDtypeStruct((n, d), F32),
        compiler_params=_cparams(),
        name="out_proj_residual",
    )(x, mods, *acts, w)


def _shortconv_kernel(bg_ref, cg_ref, xh_ref, w_ref, o_ref, *, shift):
    rows = bg_ref.shape[0]
    row = lax.broadcasted_iota(I32, (rows, 1), 0)
    u = cg_ref[...] * xh_ref[...]
    up = jnp.where(row >= shift, pltpu.roll(u, shift, 0), 0.0)
    un = jnp.where(row < rows - shift, pltpu.roll(u, rows - shift, 0), 0.0)
    y = up * w_ref[0:1, :] + u * w_ref[1:2, :] + un * w_ref[2:3, :]
    o_ref[...] = (bg_ref[...] * y).astype(BF16)


def _shortconv(proj, w_conv, n_seq, seq, shift, row_block_off):
    dc = w_conv.shape[1]
    tc = 512
    ncb = dc // tc
    kern = functools.partial(_shortconv_kernel, shift=shift)
    return pl.pallas_call(
        kern,
        grid=(n_seq, ncb),
        in_specs=[pl.BlockSpec((seq, tc), lambda b, j: (b + row_block_off, j)),
                  pl.BlockSpec((seq, tc), lambda b, j: (b + row_block_off, ncb + j)),
                  pl.BlockSpec((seq, tc), lambda b, j: (b + row_block_off, 2 * ncb + j)),
                  pl.BlockSpec((3, tc), lambda b, j: (0, j))],
        out_specs=pl.BlockSpec((seq, tc), lambda b, j: (b, j)),
        out_shape=jax.ShapeDtypeStruct((n_seq * seq, dc), BF16),
        compiler_params=_cparams(2),
        name="shortconv_mixer",
    )(proj, proj, proj, w_conv)


def _router_kernel(x_ref, mod_ref, nw_ref, rw_ref, rb_ref,
                   h_ref, ti_ref, tw_ref, rank_ref, cnt_ref, cnt_s):
    i = pl.program_id(0)
    tm = x_ref.shape[0]

    @pl.when(i == 0)
    def _():
        cnt_s[...] = jnp.zeros_like(cnt_s)

    h = _norm_mod(x_ref[...], nw_ref[...], mod_ref[3:4, :], mod_ref[4:5, :])
    h_ref[...] = h

    rw = rw_ref[...]
    h_hi = h.astype(BF16)
    h_lo = (h - h_hi.astype(F32)).astype(BF16)
    w_hi = rw.astype(BF16)
    w_lo = (rw - w_hi.astype(F32)).astype(BF16)
    logits = _dot(h_hi, w_hi) + (_dot(h_hi, w_lo) + _dot(h_lo, w_hi)) + rb_ref[...]

    lane = lax.broadcasted_iota(I32, (tm, LANES), 1)
    lane_f = lane.astype(F32)
    neg = jnp.float32(-jnp.inf)
    cur = jnp.where(lane < N_EXPERTS, logits, neg)
    vals, idxs = [], []
    for _ in range(TOP_K):
        m = jnp.max(cur, axis=-1, keepdims=True)
        idx = jnp.min(jnp.where(cur == m, lane_f, float(LANES)), axis=-1, keepdims=True).astype(I32)
        vals.append(m)
        idxs.append(idx)
        cur = jnp.where(lane == idx, neg, cur)
    exps = [jnp.exp(v - vals[0]) for v in vals]
    denom = exps[0] + exps[1] + exps[2] + exps[3]

    onehot = jnp.zeros((tm, LANES), F32)
    for idx in idxs:
        onehot = onehot + jnp.where(lane == idx, 1.0, 0.0)
    ri = lax.broadcasted_iota(I32, (tm, tm), 0)
    ci = lax.broadcasted_iota(I32, (tm, tm), 1)
    tri = jnp.where(ri > ci, 1.0, 0.0).astype(BF16)
    before = _dot(tri, onehot.astype(BF16)) + cnt_s[...]

    ti = jnp.zeros((tm, LANES), I32)
    tw = jnp.zeros((tm, LANES), F32)
    rk = jnp.zeros((tm, LANES), F32)
    for k in range(TOP_K):
        sel = lane == k
        ti = jnp.where(sel, idxs[k], ti)
        tw = jnp.where(sel, exps[k] / denom, tw)
        r = jnp.sum(jnp.where(lane == idxs[k], before, 0.0), axis=-1, keepdims=True)
        rk = jnp.where(sel, r, rk)
    ti_ref[...] = ti
    tw_ref[...] = tw
    rank_ref[...] = rk.astype(I32)
    cnt_s[...] = cnt_s[...] + jnp.sum(onehot, axis=0, keepdims=True)
    cnt_ref[...] = jnp.broadcast_to(cnt_s[...], cnt_ref.shape)


def _router(x, mods, nw, rw_pad, rb_pad, cond_of):
    n, d = x.shape
    tile = pl.BlockSpec((ROW_TILE, LANES), lambda i: (i, 0))
    return pl.pallas_call(
        _router_kernel,
        grid=(n // ROW_TILE,),
        in_specs=[pl.BlockSpec((ROW_TILE, d), lambda i: (i, 0)),
                  pl.BlockSpec((None, 6, d), lambda i: (cond_of(i), 0, 0)),
                  pl.BlockSpec((1, d), lambda i: (0, 0)),
                  pl.BlockSpec((d, LANES), lambda i: (0, 0)),
                  pl.BlockSpec((1, LANES), lambda i: (0, 0))],
        out_specs=[pl.BlockSpec((ROW_TILE, d), lambda i: (i, 0)), tile, tile, tile,
                   pl.BlockSpec((8, LANES), lambda i: (0, 0))],
        out_shape=[jax.ShapeDtypeStruct((n, d), F32),
                   jax.ShapeDtypeStruct((n, LANES), I32),
                   jax.ShapeDtypeStruct((n, LANES), F32),
                   jax.ShapeDtypeStruct((n, LANES), I32),
                   jax.ShapeDtypeStruct((8, LANES), F32)],
        scratch_shapes=[pltpu.VMEM((1, LANES), F32)],
        compiler_params=_cparams(),
        name="moe_router",
    )(x, mods, nw, rw_pad, rb_pad)


def _dispatch_plan(top_i, rank, counts, n_tok):
    n_pairs = n_tok * TOP_K
    n_slots = n_pairs + N_EXPERTS * MOE_TILE
    n_tiles = n_slots // MOE_TILE
    cnt = counts.astype(I32)
    padded = ((cnt + MOE_TILE - 1) // MOE_TILE) * MOE_TILE
    off_pad = jnp.cumsum(padded) - padded
    e_ids = jnp.arange(N_EXPERTS, dtype=I32)
    pos = jnp.sum(jnp.where(top_i[..., None] == e_ids, off_pad, 0), axis=-1) + rank

    tok_bits = int(n_tok).bit_length()
    tok_ids = jnp.arange(n_tok, dtype=I32)[:, None]
    real_keys = (top_i << (tok_bits + 1)) | tok_ids
    n_dummy = n_slots - n_pairs
    cpad = jnp.cumsum(padded - cnt)
    j = jnp.arange(n_dummy, dtype=I32)
    e_dummy = jnp.sum((j[:, None] >= cpad[None, :]).astype(I32), axis=1)
    dummy_keys = (e_dummy << (tok_bits + 1)) | (1 << tok_bits) | j
    keys = jnp.sort(jnp.concatenate([real_keys.reshape(-1), dummy_keys]))
    tok = keys & ((1 << (tok_bits + 1)) - 1)
    tok = jnp.where(tok >= n_tok, 0, tok)
    n_valid_tiles = jnp.sum(padded) // MOE_TILE
    tile_e = keys[::MOE_TILE] >> (tok_bits + 1)
    last_e = tile_e[jnp.maximum(n_valid_tiles - 1, 0)]
    tile_e = jnp.where(jnp.arange(n_tiles) < n_valid_tiles, tile_e, last_e)
    return tok.astype(I32), pos.astype(I32), tile_e.astype(I32), n_valid_tiles.astype(I32).reshape(1)


def _gather_kernel(nvt_ref, tok_ref, h_hbm, o_ref, buf, sem):
    t = pl.program_id(0)
    rows = o_ref.shape[0]

    def row_copy(r, n):
        return pltpu.make_async_copy(h_hbm.at[pl.ds(n, 1), :], buf.at[pl.ds(r, 1), :], sem)

    @pl.when(t < nvt_ref[0])
    def _():
        def issue(r, c):
            row_copy(r, tok_ref[0, 0, r]).start()
            return c
        lax.fori_loop(0, rows, issue, 0)

        def drain(r, c):
            row_copy(r, 0).wait()
            return c
        lax.fori_loop(0, rows, drain, 0)
        o_ref[...] = buf[...].astype(BF16)

    @pl.when(t >= nvt_ref[0])
    def _():
        o_ref[...] = jnp.zeros_like(o_ref)


def _gather_rows(h, tok, n_valid_tiles):
    n, d = h.shape
    n_tiles = tok.shape[0] // MOE_TILE
    return pl.pallas_call(
        _gather_kernel,
        grid_spec=pltpu.PrefetchScalarGridSpec(
            num_scalar_prefetch=1,
            grid=(n_tiles,),
            in_specs=[pl.BlockSpec((1, 1, MOE_TILE), lambda t, nv: (t, 0, 0), memory_space=pltpu.SMEM),
                      pl.BlockSpec(memory_space=pl.ANY)],
            out_specs=pl.BlockSpec((MOE_TILE, d), lambda t, nv: (t, 0)),
            scratch_shapes=[pltpu.VMEM((MOE_TILE, d), F32), pltpu.SemaphoreType.DMA(())]),
        out_shape=jax.ShapeDtypeStruct((n_tiles * MOE_TILE, d), BF16),
        compiler_params=_cparams(),
        name="moe_gather",
    )(n_valid_tiles, tok.reshape(n_tiles, 1, MOE_TILE), h)


def _experts_kernel(te_ref, nvt_ref, x_ref, wgu_ref, bgu_ref, wd_ref, bd_ref, o_ref, wgu_b, wd_b):
    t = pl.program_id(0)
    d_ff = wd_ref.shape[0]

    @pl.when(t < nvt_ref[0])
    def _():
        prev = te_ref[jnp.maximum(t - 1, 0)]

        @pl.when((t == 0) | (te_ref[t] != prev))
        def _():
            wgu_b[...] = wgu_ref[...].astype(BF16)
            wd_b[...] = wd_ref[...].astype(BF16)

        gu = _dot(x_ref[...], wgu_b[...]) + bgu_ref[...]
        gte = jnp.minimum(gu[:, :d_ff], SWIGLU_LIMIT)
        up = jnp.clip(gu[:, d_ff:], -SWIGLU_LIMIT, SWIGLU_LIMIT)
        act = (up + 1.0) * gte * _sigmoid(SWIGLU_ALPHA * gte)
        o_ref[...] = _dot(act.astype(BF16), wd_b[...]) + bd_ref[...]

    @pl.when(t >= nvt_ref[0])
    def _():
        o_ref[...] = jnp.zeros_like(o_ref)


def _experts(xs, tile_e, n_valid_tiles, wgu, bgu, wd, bd):
    n_slots, d = xs.shape
    n_e, _, n_gu = wgu.shape
    d_ff = wd.shape[1]
    return pl.pallas_call(
        _experts_kernel,
        grid_spec=pltpu.PrefetchScalarGridSpec(
            num_scalar_prefetch=2,
            grid=(n_slots // MOE_TILE,),
            in_specs=[pl.BlockSpec((MOE_TILE, d), lambda t, te, nv: (t, 0)),
                      pl.BlockSpec((None, d, n_gu), lambda t, te, nv: (te[t], 0, 0)),
                      pl.BlockSpec((None, 1, n_gu), lambda t, te, nv: (te[t], 0, 0)),
                      pl.BlockSpec((None, d_ff, d), lambda t, te, nv: (te[t], 0, 0)),
                      pl.BlockSpec((None, 1, d), lambda t, te, nv: (te[t], 0, 0))],
            out_specs=pl.BlockSpec((MOE_TILE, d), lambda t, te, nv: (t, 0)),
            scratch_shapes=[pltpu.VMEM((d, n_gu), BF16), pltpu.VMEM((d_ff, d), BF16)]),
        out_shape=jax.ShapeDtypeStruct((n_slots, d), F32),
        compiler_params=_cparams(),
        name="moe_experts",
    )(tile_e, n_valid_tiles, xs, wgu, bgu.reshape(n_e, 1, n_gu), wd, bd.reshape(n_e, 1, d))


def _combine_kernel(pos_ref, x_ref, mod_ref, tw_ref, fw_ref, y_hbm, o_ref, buf, sem, *, final):
    rows = x_ref.shape[0]

    def row_copy(i, p):
        k = i % TOP_K
        r = i // TOP_K
        return pltpu.make_async_copy(y_hbm.at[pl.ds(p, 1), :], buf.at[k, pl.ds(r, 1), :], sem)

    def issue(i, c):
        row_copy(i, pos_ref[0, 0, i]).start()
        return c
    lax.fori_loop(0, rows * TOP_K, issue, 0)

    def drain(i, c):
        row_copy(i, 0).wait()
        return c
    lax.fori_loop(0, rows * TOP_K, drain, 0)

    tw = tw_ref[...]
    acc = tw[:, 0:1] * buf[0]
    for k in range(1, TOP_K):
        acc = acc + tw[:, k:k + 1] * buf[k]
    out = x_ref[...] + mod_ref[5:6, :] * acc
    if final:
        out = out * lax.rsqrt(jnp.mean(out * out, axis=-1, keepdims=True) + EPS) * fw_ref[...]
    o_ref[...] = out


def _combine(x, mods, tw, pos, y_sorted, final_w, cond_of, final):
    n, d = x.shape
    n_steps = n // COMBINE_TILE
    kern = functools.partial(_combine_kernel, final=final)
    return pl.pallas_call(
        kern,
        grid=(n_steps,),
        in_specs=[pl.BlockSpec((1, 1, COMBINE_TILE * TOP_K), lambda i: (i, 0, 0), memory_space=pltpu.SMEM),
                  pl.BlockSpec((COMBINE_TILE, d), lambda i: (i, 0)),
                  pl.BlockSpec((None, 6, d), lambda i: (cond_of(i), 0, 0)),
                  pl.BlockSpec((COMBINE_TILE, LANES), lambda i: (i, 0)),
                  pl.BlockSpec((1, d), lambda i: (0, 0)),
                  pl.BlockSpec(memory_space=pl.ANY)],
        out_specs=pl.BlockSpec((COMBINE_TILE, d), lambda i: (i, 0)),
        out_shape=jax.ShapeDtypeStruct((n, d), F32),
        scratch_shapes=[pltpu.VMEM((TOP_K, COMBINE_TILE, d), F32), pltpu.SemaphoreType.DMA(())],
        compiler_params=_cparams(),
        name="moe_combine",
    )(pos.reshape(n_steps, 1, COMBINE_TILE * TOP_K), x, mods, tw, final_w, y_sorted)


def kernel(x_prompt, x_sample, state_delta, c, c_ctx, w_mod, b_mod, norm_mix, norm_ffn, w_in_even, conv_qkv, a_log, dt_bias, o_norm, dw_conv, dw_bias, ln_g, ln_b, w_out_even, w_in_odd, conv_odd, w_out_odd, router_w, router_b, w_gate_up, b_gate_up, w_down, b_down, final_norm):
    batch, seq_ctx, d = x_prompt.shape
    dec_batch, seq_lat, _ = x_sample.shape
    depth = w_mod.shape[0]
    n_ctx = batch * seq_ctx
    n_lat = dec_batch * seq_lat
    n_tok = n_ctx + n_lat
    da = H_A * DK
    cb = dw_conv.shape[2]
    assert seq_ctx % ROW_TILE == 0 and seq_lat % ROW_TILE == 0 and seq_lat % GRID_W == 0
    assert n_ctx % seq_lat == 0 and (3 * da) % cb == 0 and da == cb

    x = jnp.concatenate([x_prompt.reshape(n_ctx, d), x_sample.reshape(n_lat, d)], axis=0)
    cond = jnp.concatenate([c_ctx[None, :], c, jnp.zeros((16 - 1 - dec_batch, d), F32)], axis=0)
    mods_all = _modulation(cond, w_mod, b_mod).reshape(depth, 16, 6, d)

    cond_row = _cond_index(n_ctx, seq_lat, ROW_TILE)
    cond_comb = _cond_index(n_ctx, seq_lat, COMBINE_TILE)
    s0_ctx = jnp.zeros((batch, 2, H_A, DK, DK), F32)
    new_states = []

    for l in range(depth):
        j = l // 2
        mods = mods_all[l]
        nw_mix = norm_mix[l].reshape(1, d)
        if l % 2 == 0:
            w = w_in_even[j]
            n_small = 4 * H_A
            w_cat = jnp.concatenate(
                [w[:, :4 * da], w[:, 4 * da + n_small:], w[:, 4 * da:4 * da + n_small],
                 jnp.zeros((d, LANES - n_small), F32)], axis=1).astype(BF16)
            proj = _projection(x, mods, nw_mix, w_cat, cond_row)
            gp = jnp.zeros((8, LANES), F32)
            gp = gp.at[0, :2 * H_A].set(a_log[j].reshape(-1)).at[1, :2 * H_A].set(dt_bias[j].reshape(-1))
            on = o_norm[j].reshape(1, DK)
            o_ctx, s_ctx = _delta_mixer(proj, conv_qkv[j], gp, on, s0_ctx, seq_ctx, 0)
            o_lat, _ = _delta_mixer(proj, conv_qkv[j], gp, on, state_delta[:, j], seq_lat, n_ctx // seq_lat)
            new_states.append(s_ctx)
            conf_args = (dw_conv[j], dw_bias[j].reshape(1, cb), ln_g[j].reshape(1, cb), ln_b[j].reshape(1, cb))
            cu_ctx = _conformer(proj, *conf_args, n_ctx, seq_ctx, 0)
            cu_lat = _conformer(proj, *conf_args, n_lat, GRID_W, n_ctx // ROW_TILE)
            acts = [jnp.concatenate([o_ctx, o_lat], axis=0), jnp.concatenate([cu_ctx, cu_lat], axis=0)]
            x = _out_projection(x, mods, acts, w_out_even[j].astype(BF16), cond_row)
        else:
            proj = _projection(x, mods, nw_mix, w_in_odd[j].astype(BF16), cond_row)
            a_ctx = _shortconv(proj, conv_odd[j], batch, seq_ctx, 1, 0)
            a_lat = _shortconv(proj, conv_odd[j], dec_batch, seq_lat, GRID_W, n_ctx // seq_lat)
            x = _out_projection(x, mods, [jnp.concatenate([a_ctx, a_lat], axis=0)],
                                w_out_odd[j].astype(BF16), cond_row)

        rw_pad = jnp.concatenate([router_w[l], jnp.zeros((d, LANES - N_EXPERTS), F32)], axis=1)
        rb_pad = jnp.concatenate([router_b[l], jnp.zeros((LANES - N_EXPERTS,), F32)]).reshape(1, LANES)
        h, top_i, top_w, rank, counts = _router(x, mods, norm_ffn[l].reshape(1, d), rw_pad, rb_pad, cond_row)
        tok, pos, tile_e, n_valid = _dispatch_plan(top_i[:, :TOP_K], rank[:, :TOP_K],
                                                   counts[0, :N_EXPERTS], n_tok)
        xs = _gather_rows(h, tok, n_valid)
        ys = _experts(xs, tile_e, n_valid, w_gate_up[l], b_gate_up[l], w_down[l], b_down[l])
        x = _combine(x, mods, top_w, pos, ys, final_norm.reshape(1, d), cond_comb, l == depth - 1)

    y_prompt = x[:n_ctx].reshape(batch, seq_ctx, d)
    y_sample = x[n_ctx:].reshape(dec_batch, seq_lat, d)
    return y_prompt, y_sample, jnp.stack(new_states, axis=1)
```
